```python
import math
import jax, jax.numpy as jnp
from jax import lax
import numpy as np

D_MODEL = 2048
BATCH = 8
SEQ = 4096
DEPTH = 4

GRID_W = 64
CTX_LEN = 256
N_MIXERS = 3
RMS_EPS = 1e-6

HG_HEADS = 16
HG_DK = D_MODEL // HG_HEADS
HG_DV = D_MODEL // HG_HEADS
HG_CHUNK = 64

S5_GROUP = 16
S5_GROUPS = D_MODEL // S5_GROUP
S5_STATE = 64
S5_CHUNK = 128
S5_DT_MIN = 1e-3
S5_DT_MAX = 1e-1

ATT_HEADS = 32
ATT_KV_HEADS = 8
ATT_HEAD_DIM = 64
ATT_WINDOW = 128
ATT_BLOCK = 128
ROPE_BASE = 10000.0

MOE_GROUPS = 4
MOE_EXPERTS_PER_GROUP = 8
MOE_EXPERTS = MOE_GROUPS * MOE_EXPERTS_PER_GROUP
MOE_TOP_K = 2
MOE_D_FF = 512

N_LAYERS_A = (DEPTH + 2) // 3
N_LAYERS_B = (DEPTH + 1) // 3
N_LAYERS_C = DEPTH // 3

kernel_name = "hybrid_hgrn2_s5_swa_hmoe_dit"


def _rmsnorm(x, g):
    x32 = x.astype(jnp.float32)
    y = x32 * lax.rsqrt(jnp.mean(x32 * x32, axis=-1, keepdims=True) + RMS_EPS)
    return (y * g.astype(jnp.float32)).astype(x.dtype)


def _modulate(h, shift, scale):
    return h * (1 + scale) + shift


def _hgrn_scan(q, k, v, g, s0, readout):
    bsz, length, heads, _ = q.shape
    n_chunks = length // HG_CHUNK

    def to_chunks(t):
        return t.reshape(bsz, n_chunks, HG_CHUNK, heads, t.shape[-1]).transpose(1, 0, 3, 2, 4)

    causal = jnp.tril(jnp.ones((HG_CHUNK, HG_CHUNK), dtype=bool))

    def body(state, blk):
        qc, kc, vc, gc = blk
        b = jnp.cumsum(gc, axis=2)
        b_last = b[:, :, -1:, :]
        k_end = kc * jnp.exp(b_last - b)
        new_state = jnp.exp(b_last[:, :, 0, :])[..., None] * state + jnp.einsum('bhsk,bhsv->bhkv', k_end, vc)
        if not readout:
            return new_state, None
        diff = b[:, :, :, None, :] - b[:, :, None, :, :]
        decay = jnp.exp(jnp.where(causal[:, :, None], diff, -jnp.inf))
        scores = jnp.einsum('bhtk,bhsk,bhtsk->bhts', qc, kc, decay)
        o = jnp.einsum('bhts,bhsv->bhtv', scores, vc) + jnp.einsum('bhtk,bhkv->bhtv', qc * jnp.exp(b), state)
        return new_state, o

    final, o = lax.scan(body, s0, (to_chunks(q), to_chunks(k), to_chunks(v), to_chunks(g)))
    if readout:
        o = o.transpose(1, 0, 3, 2, 4).reshape(bsz, length, heads, -1)
    return o, final


def _hgrn2_mixer(hl, hc, w_in, w_out, gnorm_g, lb, with_ctx):
    f32 = jnp.float32
    bsz = hl.shape[0]
    lb = lb.reshape(HG_HEADS, HG_DK)
    log_lb = jnp.log(lb)
    log_keep = jnp.log1p(-lb)
    keep = 1.0 - lb

    def prep(h):
        n = h.shape[1]
        q, i, zf, zb, og = jnp.split(h @ w_in, 5, axis=-1)
        heads = (bsz, n, HG_HEADS, HG_DK)
        q = jax.nn.silu(q.astype(f32)).reshape(heads) * (HG_DK ** -0.5)
        v = i.astype(f32).reshape(bsz, n, HG_HEADS, HG_DV)
        dirs = []
        for z in (zf, zb):
            z = z.astype(f32).reshape(heads)
            log_f = jnp.logaddexp(log_lb, log_keep + jax.nn.log_sigmoid(z))
            k = keep * jax.nn.sigmoid(-z)
            dirs.append((k, log_f))
        return q, v, dirs, og

    def run(q, v, kg, s0, readout, reverse):
        k, g = kg
        if reverse:
            q, k, v, g = (jnp.flip(t, axis=1) for t in (q, k, v, g))
        o, s_final = _hgrn_scan(q, k, v, g, s0, readout)
        if readout and reverse:
            o = jnp.flip(o, axis=1)
        return o, s_final

    def readout_proj(o, og):
        n = o.shape[1]
        o = o * lax.rsqrt(jnp.mean(o * o, axis=-1, keepdims=True) + RMS_EPS) * gnorm_g.astype(f32)
        o = o.reshape(bsz, n, HG_HEADS * HG_DV).astype(hl.dtype) * jax.nn.silu(og)
        return o @ w_out

    qc, vc, dc, ogc = prep(hc)
    ql, vl, dl, ogl = prep(hl)
    zero = jnp.zeros((bsz, HG_HEADS, HG_DK, HG_DV), f32)
    oc_f, sc_f = run(qc, vc, dc[0], zero, with_ctx, False)
    oc_b, sc_b = run(qc, vc, dc[1], zero, with_ctx, True)
    ol_f, _ = run(ql, vl, dl[0], sc_f, True, False)
    ol_b, _ = run(ql, vl, dl[1], sc_b, True, True)
    out_l = readout_proj(ol_f + ol_b, ogl)
    out_c = readout_proj(oc_f + oc_b, ogc) if with_ctx else None
    return out_l, out_c


def _s5_zoh(a_re, a_im, log_dt, b_re, b_im):
    f32 = jnp.float32
    a_re, a_im = a_re.astype(f32), a_im.astype(f32)
    b_re, b_im = b_re.astype(f32), b_im.astype(f32)
    dt = jnp.exp(log_dt.astype(f32))[:, None]
    dta_re, dta_im = dt * a_re, dt * a_im
    mag = jnp.exp(dta_re)
    num_re = mag * jnp.cos(dta_im) - 1.0
    num_im = mag * jnp.sin(dta_im)
    den = a_re * a_re + a_im * a_im
    z_re = (num_re * a_re + num_im * a_im) / den
    z_im = (num_im * a_re - num_re * a_im) / den
    bb_re = z_re[..., None] * b_re - z_im[..., None] * b_im
    bb_im = z_re[..., None] * b_im + z_im[..., None] * b_re
    return dta_re, dta_im, bb_re, bb_im


def _complex_affine_combine(e1, e2):
    a1r, a1i, b1r, b1i = e1
    a2r, a2i, b2r, b2i = e2
    return (a2r * a1r - a2i * a1i, a2r * a1i + a2i * a1r,
            a2r * b1r - a2i * b1i + b2r, a2r * b1i + a2i * b1r + b2i)


def _s5_scan(u, dta_re, dta_im, bb_re, bb_im, c_re, c_im, x0_re, x0_im, readout):
    bsz, length = u.shape[:2]
    n_chunks = length // S5_CHUNK
    uc = u.reshape(bsz, n_chunks, S5_CHUNK, S5_GROUPS, S5_GROUP).transpose(1, 0, 2, 3, 4)
    mag = jnp.exp(dta_re)
    shape = (bsz, S5_CHUNK, S5_GROUPS, S5_STATE)
    a_re = jnp.broadcast_to(mag * jnp.cos(dta_im), shape)
    a_im = jnp.broadcast_to(mag * jnp.sin(dta_im), shape)

    def body(carry, u_blk):
        x0r, x0i = carry
        bu_re = jnp.einsum('bcgh,gph->bcgp', u_blk, bb_re)
        bu_im = jnp.einsum('bcgh,gph->bcgp', u_blk, bb_im)
        ar, ai, xr, xi = lax.associative_scan(_complex_affine_combine, (a_re, a_im, bu_re, bu_im), axis=1)
        xr_full = ar * x0r[:, None] - ai * x0i[:, None] + xr
        xi_full = ar * x0i[:, None] + ai * x0r[:, None] + xi
        new = (xr_full[:, -1], xi_full[:, -1])
        if not readout:
            return new, None
        y = jnp.einsum('bcgp,ghp->bcgh', xr_full, c_re) - jnp.einsum('bcgp,ghp->bcgh', xi_full, c_im)
        return new, y

    final, y = lax.scan(body, (x0_re, x0_im), uc)
    if readout:
        y = y.transpose(1, 0, 2, 3, 4).reshape(bsz, length, S5_GROUPS, S5_GROUP)
    return y, final


def _s5_mixer(hl, hc, a_re, a_im, log_dt, b_re, b_im, c_re, c_im, d_skip, w_glu, with_ctx):
    f32 = jnp.float32
    bsz = hl.shape[0]
    params = [_s5_zoh(a_re[d], a_im[d], log_dt[d], b_re, b_im) for d in range(2)]
    c_re, c_im = c_re.astype(f32), c_im.astype(f32)
    d_g = d_skip.astype(f32).reshape(S5_GROUPS, S5_GROUP)

    def groups(h):
        return h.astype(f32).reshape(bsz, h.shape[1], S5_GROUPS, S5_GROUP)

    def run(u, p, x0, readout, reverse):
        if reverse:
            u = jnp.flip(u, axis=1)
        y, xf = _s5_scan(u, *p, c_re, c_im, x0[0], x0[1], readout)
        if readout and reverse:
            y = jnp.flip(y, axis=1)
        return y, xf

    def glu_out(y_f, y_b, u):
        n = u.shape[1]
        y = (y_f + y_b + d_g * u).reshape(bsz, n, D_MODEL).astype(hl.dtype)
        a, g = jnp.split(y @ w_glu, 2, axis=-1)
        return a * jax.nn.sigmoid(g)

    uc, ul = groups(hc), groups(hl)
    zero = (jnp.zeros((bsz, S5_GROUPS, S5_STATE), f32), jnp.zeros((bsz, S5_GROUPS, S5_STATE), f32))
    yc_f, xc_f = run(uc, params[0], zero, with_ctx, False)
    yc_b, xc_b = run(uc, params[1], zero, with_ctx, True)
    yl_f, _ = run(ul, params[0], xc_f, True, False)
    yl_b, _ = run(ul, params[1], xc_b, True, True)
    out_l = glu_out(yl_f, yl_b, ul)
    out_c = glu_out(yc_f, yc_b, uc) if with_ctx else None
    return out_l, out_c


def _axial_rope(t, rows, cols):
    half = ATT_HEAD_DIM // 2
    quarter = half // 2
    inv_freq = ROPE_BASE ** (-jnp.arange(quarter, dtype=jnp.float32) / quarter)

    def rot(th, pos):
        ang = pos.astype(jnp.float32)[:, None] * inv_freq
        cos = jnp.cos(ang)[None, :, None, :].astype(t.dtype)
        sin = jnp.sin(ang)[None, :, None, :].astype(t.dtype)
        t1, t2 = th[..., :quarter], th[..., quarter:]
        return jnp.concatenate([t1 * cos - t2 * sin, t1 * sin + t2 * cos], axis=-1)

    return jnp.concatenate([rot(t[..., :half], rows), rot(t[..., half:], cols)], axis=-1)


def _sink_softmax(s, sink):
    sk = sink[None, :, :, None, None]
    m = jnp.maximum(jnp.max(s, axis=-1, keepdims=True), sk)
    e = jnp.exp(s - m)
    return e / (jnp.sum(e, axis=-1, keepdims=True) + jnp.exp(sk - m))


def _attn_mixer(hl, hc, w_qkv, w_o, sink, rows, cols, with_ctx):
    bsz, length, _ = hl.shape
    G, R, HD = ATT_KV_HEADS, ATT_HEADS // ATT_KV_HEADS, ATT_HEAD_DIM
    scale = HD ** -0.5
    sink = sink.astype(jnp.float32).reshape(G, R)

    def proj(h):
        n = h.shape[1]
        q, k, v = jnp.split(h @ w_qkv, [ATT_HEADS * HD, (ATT_HEADS + G) * HD], axis=-1)
        return q.reshape(bsz, n, ATT_HEADS, HD), k.reshape(bsz, n, G, HD), v.reshape(bsz, n, G, HD)

    ql, kl, vl = proj(hl)
    qc, kc, vc = proj(hc)
    n_ctx = hc.shape[1]
    ql = _axial_rope(ql, rows, cols).reshape(bsz, length, G, R, HD)
    kl = _axial_rope(kl, rows, cols)
    pad = ((0, 0), (ATT_BLOCK, ATT_BLOCK), (0, 0), (0, 0))
    k_pad, v_pad = jnp.pad(kl, pad), jnp.pad(vl, pad)
    offs_q = jnp.arange(ATT_BLOCK)
    offs_k = jnp.arange(3 * ATT_BLOCK) - ATT_BLOCK

    def block(n):
        start = n * ATT_BLOCK
        q_blk = lax.dynamic_slice_in_dim(ql, start, ATT_BLOCK, axis=1)
        k_win = lax.dynamic_slice_in_dim(k_pad, start, 3 * ATT_BLOCK, axis=1)
        v_win = lax.dynamic_slice_in_dim(v_pad, start, 3 * ATT_BLOCK, axis=1)
        q_pos = start + offs_q
        k_pos = start + offs_k
        valid = (k_pos[None, :] >= 0) & (k_pos[None, :] < length) & (jnp.abs(q_pos[:, None] - k_pos[None, :]) <= ATT_WINDOW)
        s_loc = jnp.einsum('bqgrd,bkgd->bgrqk', q_blk, k_win).astype(jnp.float32) * scale
        s_loc = jnp.where(valid, s_loc, -jnp.inf)
        s_ctx = jnp.einsum('bqgrd,bkgd->bgrqk', q_blk, kc).astype(jnp.float32) * scale
        p = _sink_softmax(jnp.concatenate([s_ctx, s_loc], axis=-1), sink).astype(hl.dtype)
        return (jnp.einsum('bgrqk,bkgd->bqgrd', p[..., :n_ctx], vc)
                + jnp.einsum('bgrqk,bkgd->bqgrd', p[..., n_ctx:], v_win))

    o = lax.map(block, jnp.arange(length // ATT_BLOCK))
    o = o.transpose(1, 0, 2, 3, 4, 5).reshape(bsz, length, ATT_HEADS * HD)
    out_l = o @ w_o
    out_c = None
    if with_ctx:
        qc = qc.reshape(bsz, n_ctx, G, R, HD)
        s = jnp.einsum('bqgrd,bkgd->bgrqk', qc, kc).astype(jnp.float32) * scale
        p = _sink_softmax(s, sink).astype(hc.dtype)
        oc = jnp.einsum('bgrqk,bkgd->bqgrd', p, vc).reshape(bsz, n_ctx, ATT_HEADS * HD)
        out_c = oc @ w_o
    return out_l, out_c


def _hier_moe(h, w_group, b_group, w_expert, b_expert, w_gate_up, w_down):
    f32 = jnp.float32
    h32 = h.astype(f32)
    p_group = jax.nn.softmax(h32 @ w_group.astype(f32) + b_group.astype(f32), axis=-1)
    g_sel = jnp.argmax(p_group, axis=-1)
    p_sel = jnp.max(p_group, axis=-1)
    logits = (h32 @ w_expert.astype(f32) + b_expert.astype(f32)).reshape(
        h.shape[:-1] + (MOE_GROUPS, MOE_EXPERTS_PER_GROUP))
    logits_g = jnp.take_along_axis(logits, g_sel[..., None, None], axis=-2)[..., 0, :]
    top_v, top_i = lax.top_k(logits_g, MOE_TOP_K)
    w_sel = jax.nn.softmax(top_v, axis=-1) * p_sel[..., None]
    expert_id = g_sel[..., None] * MOE_EXPERTS_PER_GROUP + top_i
    gates = jnp.sum(jax.nn.one_hot(expert_id, MOE_EXPERTS, dtype=f32) * w_sel[..., None], axis=-2).astype(h.dtype)
    out = jnp.zeros_like(h)
    for e in range(MOE_EXPERTS):
        a, b = jnp.split(h @ w_gate_up[e], 2, axis=-1)
        out = out + gates[..., e:e + 1] * ((jax.nn.silu(a) * b) @ w_down[e])
    return out


def setup_inputs(seed: int = 0) -> dict:
    key = jax.random.key(seed)
    keys = iter(jax.random.split(key, 32))
    f32 = jnp.float32
    D = D_MODEL

    def normal(shape, std):
        return std * jax.random.normal(next(keys), shape, f32)

    G, P, CH = S5_GROUPS, S5_STATE, S5_GROUP
    qkv_width = (ATT_HEADS + 2 * ATT_KV_HEADS) * ATT_HEAD_DIM
    x = normal((BATCH, SEQ, D), 1.0)
    c = normal((BATCH, D), 1.0)
    ctx = normal((BATCH, CTX_LEN, D), 1.0)
    c_ctx = normal((D,), 1.0)
    ada_w = normal((DEPTH, D, 6 * D), 0.5 * D ** -0.5)
    ada_b = normal((DEPTH, 6 * D), 0.02)
    norm_mix_g = 1.0 + normal((DEPTH, D), 0.05)
    norm_ffn_g = 1.0 + normal((DEPTH, D), 0.05)
    final_norm_g = 1.0 + normal((D,), 0.05)
    hgrn_w_in = normal((N_LAYERS_A, D, 5 * D), D ** -0.5)
    hgrn_w_out = normal((N_LAYERS_A, D, D), D ** -0.5)
    hgrn_gnorm_g = 1.0 + normal((N_LAYERS_A, HG_DV), 0.05)
    hgrn_lb_logits = normal((DEPTH, D), 1.0)
    s5_a_re = -0.5 * jnp.exp(normal((N_LAYERS_B, 2, G, P), 0.05))
    s5_a_im = math.pi * jnp.arange(P, dtype=f32) + normal((N_LAYERS_B, 2, G, P), 0.05)
    s5_log_dt = jax.random.uniform(next(keys), (N_LAYERS_B, 2, G), f32,
                                   minval=math.log(S5_DT_MIN), maxval=math.log(S5_DT_MAX))
    s5_b_re = normal((N_LAYERS_B, G, P, CH), (2 * CH) ** -0.5)
    s5_b_im = normal((N_LAYERS_B, G, P, CH), (2 * CH) ** -0.5)
    s5_c_re = normal((N_LAYERS_B, G, CH, P), 0.5)
    s5_c_im = normal((N_LAYERS_B, G, CH, P), 0.5)
    s5_d = normal((N_LAYERS_B, D), 1.0)
    s5_w_glu = normal((N_LAYERS_B, D, 2 * D), D ** -0.5)
    attn_w_qkv = normal((N_LAYERS_C, D, qkv_width), D ** -0.5)
    attn_w_o = normal((N_LAYERS_C, ATT_HEADS * ATT_HEAD_DIM, D), (ATT_HEADS * ATT_HEAD_DIM) ** -0.5)
    attn_sink = normal((N_LAYERS_C, ATT_HEADS), 1.0)
    moe_w_group = normal((DEPTH, D, MOE_GROUPS), D ** -0.5)
    moe_b_group = normal((DEPTH, MOE_GROUPS), 0.01)
    moe_w_expert = normal((DEPTH, D, MOE_EXPERTS), D ** -0.5)
    moe_b_expert = normal((DEPTH, MOE_EXPERTS), 0.01)
    moe_w_gate_up = normal((DEPTH, MOE_EXPERTS, D, 2 * MOE_D_FF), D ** -0.5)
    moe_w_down = normal((DEPTH, MOE_EXPERTS, MOE_D_FF, D), MOE_D_FF ** -0.5)
    return {"x": x, "c": c, "ctx": ctx, "c_ctx": c_ctx,
            "ada_w": ada_w, "ada_b": ada_b, "norm_mix_g": norm_mix_g, "norm_ffn_g": norm_ffn_g,
            "final_norm_g": final_norm_g,
            "hgrn_w_in": hgrn_w_in, "hgrn_w_out": hgrn_w_out, "hgrn_gnorm_g": hgrn_gnorm_g,
            "hgrn_lb_logits": hgrn_lb_logits,
            "s5_a_re": s5_a_re, "s5_a_im": s5_a_im, "s5_log_dt": s5_log_dt, "s5_b_re": s5_b_re,
            "s5_b_im": s5_b_im, "s5_c_re": s5_c_re, "s5_c_im": s5_c_im, "s5_d": s5_d, "s5_w_glu": s5_w_glu,
            "attn_w_qkv": attn_w_qkv, "attn_w_o": attn_w_o, "attn_sink": attn_sink,
            "moe_w_group": moe_w_group, "moe_b_group": moe_b_group, "moe_w_expert": moe_w_expert,
            "moe_b_expert": moe_b_expert, "moe_w_gate_up": moe_w_gate_up, "moe_w_down": moe_w_down}


def reference(x, c, ctx, c_ctx, ada_w, ada_b, norm_mix_g, norm_ffn_g, final_norm_g,
              hgrn_w_in, hgrn_w_out, hgrn_gnorm_g, hgrn_lb_logits,
              s5_a_re, s5_a_im, s5_log_dt, s5_b_re, s5_b_im, s5_c_re, s5_c_im, s5_d, s5_w_glu,
              attn_w_qkv, attn_w_o, attn_sink,
              moe_w_group, moe_b_group, moe_w_expert, moe_b_expert, moe_w_gate_up, moe_w_down):
    seq = x.shape[1]
    n_ctx = ctx.shape[1]
    ROWS = seq // GRID_W
    rows = jnp.repeat(jnp.arange(ROWS), GRID_W)
    cols = jnp.tile(jnp.arange(GRID_W), ROWS)
    lb_cum = jnp.cumsum(jax.nn.softmax(hgrn_lb_logits.astype(jnp.float32), axis=0), axis=0)
    lower_bounds = lb_cum - lb_cum[:1]
    silu_c = jax.nn.silu(c)
    silu_cc = jax.nn.silu(c_ctx)
    xl, xc = x, ctx
    for i in range(DEPTH):
        with_ctx = i < DEPTH - 1
        mod_l = (silu_c @ ada_w[i] + ada_b[i])[:, None, :]
        mod_c = silu_cc @ ada_w[i] + ada_b[i]
        sh1_l, sc1_l, gt1_l, sh2_l, sc2_l, gt2_l = jnp.split(mod_l, 6, axis=-1)
        sh1_c, sc1_c, gt1_c, sh2_c, sc2_c, gt2_c = jnp.split(mod_c, 6, axis=-1)
        hl = _modulate(_rmsnorm(xl, norm_mix_g[i]), sh1_l, sc1_l)
        hc = _modulate(_rmsnorm(xc, norm_mix_g[i]), sh1_c, sc1_c)
        kind, j = i % N_MIXERS, i // N_MIXERS
        if kind == 0:
            ol, oc = _hgrn2_mixer(hl, hc, hgrn_w_in[j], hgrn_w_out[j], hgrn_gnorm_g[j], lower_bounds[i], with_ctx)
        elif kind == 1:
            ol, oc = _s5_mixer(hl, hc, s5_a_re[j], s5_a_im[j], s5_log_dt[j], s5_b_re[j], s5_b_im[j],
                               s5_c_re[j], s5_c_im[j], s5_d[j], s5_w_glu[j], with_ctx)
        else:
            ol, oc = _attn_mixer(hl, hc, attn_w_qkv[j], attn_w_o[j], attn_sink[j], rows, cols, with_ctx)
        xl = xl + gt1_l * ol
        hl = _modulate(_rmsnorm(xl, norm_ffn_g[i]), sh2_l, sc2_l)
        moe_args = (moe_w_group[i], moe_b_group[i], moe_w_expert[i], moe_b_expert[i], moe_w_gate_up[i], moe_w_down[i])
        if with_ctx:
            xc = xc + gt1_c * oc
            hc = _modulate(_rmsnorm(xc, norm_ffn_g[i]), sh2_c, sc2_c)
            f = _hier_moe(jnp.concatenate([hc, hl], axis=1), *moe_args)
            xc = xc + gt2_c * f[:, :n_ctx]
            xl = xl + gt2_l * f[:, n_ctx:]
        else:
            xl = xl + gt2_l * _hier_moe(hl, *moe_args)
    return _rmsnorm(xl, final_norm_g)
```

```python
import functools
import math

import numpy as np
import jax
import jax.numpy as jnp
from jax import lax
from jax.experimental import pallas as pl
from jax.experimental.pallas import tpu as pltpu

F32 = jnp.float32
BF16 = jnp.bfloat16
HIGHEST = lax.Precision.HIGHEST

RMS_EPS = 1e-6
GRID_W = 64
ROPE_BASE = 10000.0
HG_DK = 128
HG_CHUNK = 64
S5_CH = 16
S5_LC = 16
S5_DT_MIN = 1e-3
ATT_HD = 64
ATT_REP = 4
ATT_BLK = 128
MOE_TOP_K = 2
MOE_GROUPS = 4
MOE_TM = 256
VMEM_LIMIT = 56 * 1024 * 1024


def _cparams(n_axes):
    return pltpu.CompilerParams(dimension_semantics=("arbitrary",) * n_axes,
                                vmem_limit_bytes=VMEM_LIMIT)


def _dot(a, b):
    return jnp.dot(a, b, preferred_element_type=F32)


def _dot_nt(a, b):
    return lax.dot_general(a, b, (((1,), (1,)), ((), ())), preferred_element_type=F32)


def _silu(x):
    return x * jax.nn.sigmoid(x)


def _row_is_ctx(tile_idx, tm, n_ctx):
    row = tile_idx * tm + lax.broadcasted_iota(jnp.int32, (tm, 1), 0)
    return row < n_ctx


def _adaln_kernel(c_ref, w_ref, b_ref, o_ref):
    s = _silu(c_ref[...])
    o_ref[...] = jnp.dot(s, w_ref[...], precision=HIGHEST, preferred_element_type=F32) + b_ref[...]


def _adaln(c_all, ada_w, ada_b):
    depth, d, n = ada_w.shape
    r = c_all.shape[0]
    tn = min(n, 1024)
    return pl.pallas_call(
        _adaln_kernel,
        grid=(depth, n // tn),
        in_specs=[pl.BlockSpec((r, d), lambda i, j: (0, 0)),
                  pl.BlockSpec((None, d, tn), lambda i, j: (i, 0, j)),
                  pl.BlockSpec((None, 1, tn), lambda i, j: (i, 0, j))],
        out_specs=pl.BlockSpec((None, r, tn), lambda i, j: (i, 0, j)),
        out_shape=jax.ShapeDtypeStruct((depth, r, n), F32),
        compiler_params=_cparams(2),
        name="adaln",
    )(c_all, ada_w, ada_b.reshape(depth, 1, n))


def _mod_specs(n_batch, d, section, baxis, tn=None, caxis=None):
    tn = d if tn is None else tn
    per = d // tn

    def lat(*ids):
        c = 0 if caxis is None else ids[caxis]
        return (ids[baxis], 0, section * per + c)

    def ctx(*ids):
        c = 0 if caxis is None else ids[caxis]
        return (n_batch, 0, section * per + c)

    return [pl.BlockSpec((None, 1, tn), lat), pl.BlockSpec((None, 1, tn), ctx)]


def _norm_mod(x, g, shl, scl, shc, scc, is_ctx):
    y = x * lax.rsqrt(jnp.mean(x * x, axis=-1, keepdims=True) + RMS_EPS) * g
    sc = jnp.where(is_ctx, scc, scl)
    sh = jnp.where(is_ctx, shc, shl)
    return y * (1.0 + sc) + sh


def _normmod_kernel(x_ref, g_ref, shl_ref, shc_ref, scl_ref, scc_ref, o_ref, *, tm, n_ctx):
    is_ctx = _row_is_ctx(pl.program_id(1), tm, n_ctx)
    h = _norm_mod(x_ref[...], g_ref[...], shl_ref[...], scl_ref[...], shc_ref[...], scc_ref[...], is_ctx)
    o_ref[...] = h.astype(o_ref.dtype)


def _normmod(xs, g, mod, sec_shift, n_ctx, tm=256):
    b, s, d = xs.shape
    return pl.pallas_call(
        functools.partial(_normmod_kernel, tm=tm, n_ctx=n_ctx),
        grid=(b, s // tm),
        in_specs=[pl.BlockSpec((None, tm, d), lambda i, j: (i, j, 0)),
                  pl.BlockSpec((1, d), lambda i, j: (0, 0))]
                 + _mod_specs(b, d, sec_shift, 0) + _mod_specs(b, d, sec_shift + 1, 0),
        out_specs=pl.BlockSpec((None, tm, d), lambda i, j: (i, j, 0)),
        out_shape=jax.ShapeDtypeStruct((b, s, d), BF16),
        compiler_params=_cparams(2),
        name="normmod",
    )(xs, g.reshape(1, d), mod, mod, mod, mod)


def _mm_kernel(*refs, mode, tm, n_ctx):
    first = (pl.program_id(1) == 0) & (pl.program_id(2) == 0)
    if mode == "plain":
        x_ref, w_ref, o_ref, wb_ref = refs
    elif mode == "res":
        x_ref, w_ref, xs_ref, gl_ref, gc_ref, o_ref, wb_ref = refs
    else:
        x_ref, w_ref, w2_ref, xs_ref, gl_ref, gc_ref, o_ref, wb_ref, wb2_ref = refs

    @pl.when(first)
    def _():
        wb_ref[...] = w_ref[...].astype(BF16)
        if mode == "glu":
            wb2_ref[...] = w2_ref[...].astype(BF16)

    acc = _dot(x_ref[...], wb_ref[...])
    if mode == "glu":
        acc = acc * jax.nn.sigmoid(_dot(x_ref[...], wb2_ref[...]))
    if mode != "plain":
        gate = jnp.where(_row_is_ctx(pl.program_id(2), tm, n_ctx), gc_ref[...], gl_ref[...])
        acc = xs_ref[...] + gate * acc
    o_ref[...] = acc.astype(o_ref.dtype)


def _pick_tm(s):
    for tm in (1088, 1024, 768, 512, 384, 256, 128):
        if s % tm == 0:
            return tm
    return s


def _pick_tn(n, tn):
    while n % tn:
        tn //= 2
    return tn


def _matmul(x, w, *, mode="plain", xs=None, mod=None, sec_gate=None, n_ctx=0, out_dtype=F32, tn=1024):
    b, s, k = x.shape
    n = w.shape[1] // 2 if mode == "glu" else w.shape[1]
    tn = _pick_tn(n, tn // 2 if mode == "glu" else tn)
    tm = _pick_tm(s)
    nj = n // tn
    in_specs = [pl.BlockSpec((None, tm, k), lambda j, i, r: (i, r, 0)),
                pl.BlockSpec((k, tn), lambda j, i, r: (0, j))]
    args = [x, w]
    scratch = [pltpu.VMEM((k, tn), BF16)]
    if mode == "glu":
        in_specs.append(pl.BlockSpec((k, tn), lambda j, i, r: (0, j + nj)))
        args.append(w)
        scratch.append(pltpu.VMEM((k, tn), BF16))
    if mode != "plain":
        in_specs.append(pl.BlockSpec((None, tm, tn), lambda j, i, r: (i, r, j)))
        in_specs += _mod_specs(b, n, sec_gate, 1, tn=tn, caxis=0)
        args += [xs, mod, mod]
    return pl.pallas_call(
        functools.partial(_mm_kernel, mode=mode, tm=tm, n_ctx=n_ctx),
        grid=(nj, b, s // tm),
        in_specs=in_specs,
        out_specs=pl.BlockSpec((None, tm, tn), lambda j, i, r: (i, r, j)),
        out_shape=jax.ShapeDtypeStruct((b, s, n), out_dtype),
        scratch_shapes=scratch,
        compiler_params=_cparams(3),
        name="matmul_" + mode,
    )(*args)


def _hgrn_consts(c, reverse):
    t = np.arange(c)
    mats = [(t[None, :] <= t[:, None]).astype(np.float32), (t[None, :] > t[:, None]).astype(np.float32)]
    masks = []
    n = c // 2
    while n >= 1:
        blk, pos = t // (2 * n), t % (2 * n)
        upper = pos >= n
        ref = blk * 2 * n + n - 1
        a = np.zeros((c, c), np.float32)
        for i in range(c):
            if upper[i]:
                a[i, ref[i] + 1:i + 1] = 1.0
            else:
                a[i, i + 1:ref[i] + 1] = 1.0
        mats.append(a)
        masks.append(((blk[:, None] == blk[None, :]) & upper[:, None] & (~upper)[None, :]).astype(np.float32))
        n //= 2
    masks.append(np.eye(c, dtype=np.float32))
    if reverse:
        mats = [m[::-1, ::-1] for m in mats]
        masks = [m[::-1, ::-1] for m in masks]
    return np.concatenate(mats, 0), np.stack(masks)


def _hgrn_kernel(*refs, reverse, heads, c, n_levels):
    if reverse:
        (q_ref, v_ref, z_ref, loglb_ref, logkeep_ref, keep_ref, a_ref, m_ref,
         of_ref, og_ref, gn_ref, o_ref, st_ref) = refs
    else:
        q_ref, v_ref, z_ref, loglb_ref, logkeep_ref, keep_ref, a_ref, m_ref, o_ref, st_ref = refs

    @pl.when(pl.program_id(1) == 0)
    def _():
        st_ref[...] = jnp.zeros_like(st_ref)

    amat = a_ref[...]
    scale = HG_DK ** -0.5

    def head(h, carry):
        hs = pl.ds(pl.multiple_of(h * HG_DK, HG_DK), HG_DK)
        qraw = q_ref[:, hs]
        v = v_ref[:, hs]
        z = z_ref[:, hs]
        q = _silu(qraw) * scale
        log_sig = jnp.minimum(z, 0.0) - jnp.log1p(jnp.exp(-jnp.abs(z)))
        la = loglb_ref[:, hs]
        lb = logkeep_ref[:, hs] + log_sig
        g = jnp.maximum(la, lb) + jnp.log1p(jnp.exp(-jnp.abs(la - lb)))
        k = keep_ref[:, hs] * jax.nn.sigmoid(-z)
        g_hi = g.astype(BF16)
        g_lo = (g - g_hi.astype(F32)).astype(BF16)
        e_all = _dot(amat, g_hi) + _dot(amat, g_lo)
        b = e_all[0:c]
        e_after = e_all[c:2 * c]
        tot = b[0:1] if reverse else b[c - 1:c]
        sc = m_ref[n_levels] * _dot_nt(q.astype(BF16), k.astype(BF16))
        for lvl in range(n_levels):
            w = jnp.exp(jnp.minimum(e_all[(2 + lvl) * c:(3 + lvl) * c], 0.0))
            sc = sc + m_ref[lvl] * _dot_nt((q * w).astype(BF16), (k * w).astype(BF16))
        st = st_ref[h]
        o = _dot(sc.astype(BF16), v.astype(BF16)) + _dot_nt((q * jnp.exp(b)).astype(BF16), st.astype(BF16))
        k_end = (k * jnp.exp(e_after)).astype(BF16)
        st_ref[h] = st * jnp.exp(tot) + _dot(v.T.astype(BF16), k_end)
        if reverse:
            o = o + of_ref[:, hs]
            o = o * lax.rsqrt(jnp.mean(o * o, axis=-1, keepdims=True) + RMS_EPS) * gn_ref[...]
            o_ref[:, hs] = (o * _silu(og_ref[:, hs])).astype(o_ref.dtype)
        else:
            o_ref[:, hs] = o
        return carry

    lax.fori_loop(0, heads, head, 0)


def _hgrn_scan(proj, lb, gnorm, n_ctx):
    b, s, d5 = proj.shape
    d = d5 // 5
    heads = d // HG_DK
    c = HG_CHUNK
    n_levels = int(math.log2(c))
    nc, ncc = s // c, n_ctx // c
    log_lb = jnp.log(lb).reshape(1, d)
    log_keep = jnp.log1p(-lb).reshape(1, d)
    keep = (1.0 - lb).reshape(1, d)

    def chunk_fwd(j):
        return j

    def chunk_bwd(j):
        return jnp.where(j < ncc, ncc - 1 - j, ncc + nc - 1 - j)

    def run(reverse, o_fwd=None):
        cm = chunk_bwd if reverse else chunk_fwd
        amat, masks = _hgrn_consts(c, reverse)

        def sec(k):
            return pl.BlockSpec((None, c, d), lambda i, j: (i, cm(j), k))

        vec = pl.BlockSpec((1, d), lambda i, j: (0, 0))
        in_specs = [sec(0), sec(1), sec(3 if reverse else 2), vec, vec, vec,
                    pl.BlockSpec(amat.shape, lambda i, j: (0, 0)),
                    pl.BlockSpec(masks.shape, lambda i, j: (0, 0, 0))]
        args = [proj, proj, proj, log_lb, log_keep, keep, jnp.asarray(amat, BF16), jnp.asarray(masks, F32)]
        if reverse:
            in_specs += [pl.BlockSpec((None, c, d), lambda i, j: (i, cm(j), 0)), sec(4),
                         pl.BlockSpec((1, HG_DK), lambda i, j: (0, 0))]
            args += [o_fwd, proj, gnorm.reshape(1, HG_DK)]
        return pl.pallas_call(
            functools.partial(_hgrn_kernel, reverse=reverse, heads=heads, c=c, n_levels=n_levels),
            grid=(b, nc),
            in_specs=in_specs,
            out_specs=pl.BlockSpec((None, c, d), lambda i, j: (i, cm(j), 0)),
            out_shape=jax.ShapeDtypeStruct((b, s, d), BF16 if reverse else F32),
            scratch_shapes=[pltpu.VMEM((heads, HG_DK, HG_DK), F32)],
            compiler_params=_cparams(2),
            name="hgrn_bwd" if reverse else "hgrn_fwd",
        )(*args)

    return run(True, run(False))


def _s5_matrices(a_re, a_im, log_dt, b_re, b_im, c_re, c_im, d_skip):
    lc = S5_LC
    g, p = a_re.shape[1], a_re.shape[2]
    ein = functools.partial(jnp.einsum, precision=HIGHEST)
    taus = jnp.arange(lc + 1, dtype=F32)
    mats, vs, lams = [], [], []
    t = np.arange(lc)
    for d in range(2):
        are, aim = a_re[d].astype(F32), a_im[d].astype(F32)
        dt = jnp.exp(log_dt[d].astype(F32))[:, None]
        dre, dim = dt * are, dt * aim
        mag = jnp.exp(dre)
        nre, nim = mag * jnp.cos(dim) - 1.0, mag * jnp.sin(dim)
        den = are * are + aim * aim
        zre, zim = (nre * are + nim * aim) / den, (nim * are - nre * aim) / den
        bbre = zre[..., None] * b_re - zim[..., None] * b_im
        bbim = zre[..., None] * b_im + zim[..., None] * b_re
        pmag = jnp.exp(taus[:, None, None] * dre)
        pre, pim = pmag * jnp.cos(taus[:, None, None] * dim), pmag * jnp.sin(taus[:, None, None] * dim)
        lbre = pre[..., None] * bbre - pim[..., None] * bbim
        lbim = pre[..., None] * bbim + pim[..., None] * bbre
        kern = ein('ghp,tgpi->tghi', c_re, lbre[:lc]) - ein('ghp,tgpi->tghi', c_im, lbim[:lc])
        lag = (t[:, None] - t[None, :]) if d == 0 else (t[None, :] - t[:, None])
        valid = jnp.asarray(lag >= 0, F32)
        toe = kern[np.clip(lag, 0, lc - 1)] * valid[:, :, None, None, None]
        mats.append(toe.transpose(2, 1, 4, 0, 3).reshape(g, lc * S5_CH, lc * S5_CH))
        idx = (lc - 1 - t) if d == 0 else t
        wre = lbre[idx].transpose(1, 0, 3, 2).reshape(g, lc * S5_CH, p)
        wim = lbim[idx].transpose(1, 0, 3, 2).reshape(g, lc * S5_CH, p)
        e = (t + 1) if d == 0 else (lc - t)
        cre = c_re[None] * pre[e][:, :, None, :] - c_im[None] * pim[e][:, :, None, :]
        cim = c_re[None] * pim[e][:, :, None, :] + c_im[None] * pre[e][:, :, None, :]
        vre = cre.transpose(1, 3, 0, 2).reshape(g, p, lc * S5_CH)
        vim = -cim.transpose(1, 3, 0, 2).reshape(g, p, lc * S5_CH)
        vs.append((vre, vim))
        mats.append((wre, wim))
        lams.append((pre[lc], pim[lc]))
    skip = jnp.eye(lc * S5_CH, dtype=F32)[None] * jnp.tile(d_skip.astype(F32).reshape(g, 1, S5_CH), (1, 1, lc))
    local = mats[0] + mats[2] + skip
    (wfr, wfi), (wbr, wbi) = mats[1], mats[3]
    r_mat = jnp.concatenate([local, wfr, wbr, wfi, wbi], axis=-1)
    v_mat = jnp.concatenate([vs[0][0], vs[1][0], vs[0][1], vs[1][1]], axis=1)
    lam = jnp.stack([jnp.concatenate([lams[0][0], lams[1][0]], -1),
                     jnp.concatenate([lams[0][1], lams[1][1]], -1)], axis=1)
    return r_mat.astype(BF16), v_mat.astype(BF16), lam


def _s5_local_kernel(u_ref, r_ref, y_ref, w_ref):
    res = _dot(u_ref[...], r_ref[...])
    n = y_ref.shape[-1]
    y_ref[...] = res[:, :n]
    w_ref[...] = res[:, n:]


def _s5_scan_kernel(w_ref, lam_ref, x_ref, *, nb, n_chunks, n_ctx_chunks):
    gb = w_ref.shape[0]
    p2 = w_ref.shape[-1] // 2
    p = p2 // 2
    ar = lam_ref[:, 0:1, :]
    ai = lam_ref[:, 1:2, :]
    is_fwd = lax.broadcasted_iota(jnp.int32, (gb, nb, p2), 2) < p

    def step(i, carry):
        xr, xi = carry
        nf = i
        nbk = jnp.where(i < n_ctx_chunks, n_ctx_chunks - 1 - i, n_ctx_chunks + n_chunks - 1 - i)
        rf = pl.ds(pl.multiple_of(nf * nb, nb), nb)
        rb = pl.ds(pl.multiple_of(nbk * nb, nb), nb)
        x_ref[:, rf, 0:p] = xr[:, :, 0:p].astype(x_ref.dtype)
        x_ref[:, rb, p:p2] = xr[:, :, p:p2].astype(x_ref.dtype)
        x_ref[:, rf, p2:p2 + p] = xi[:, :, 0:p].astype(x_ref.dtype)
        x_ref[:, rb, p2 + p:2 * p2] = xi[:, :, p:p2].astype(x_ref.dtype)
        wf = w_ref[:, rf, :]
        wb = w_ref[:, rb, :]
        wr = jnp.where(is_fwd, wf[:, :, 0:p2], wb[:, :, 0:p2])
        wi = jnp.where(is_fwd, wf[:, :, p2:], wb[:, :, p2:])
        return ar * xr - ai * xi + wr, ar * xi + ai * xr + wi

    zero = jnp.zeros((gb, nb, p2), F32)
    lax.fori_loop(0, n_chunks, step, (zero, zero))


def _s5_out_kernel(y_ref, x_ref, v_ref, o_ref):
    o_ref[...] = (y_ref[...] + _dot(x_ref[...], v_ref[...])).astype(o_ref.dtype)


def _s5_mix(h, r_mat, v_mat, lam, n_ctx):
    b, s, d = h.shape
    g = d // S5_CH
    lc = S5_LC
    nch, ncc = s // lc, n_ctx // lc
    m = nch * b
    kw = lc * S5_CH
    p2 = lam.shape[-1]
    u = h.reshape(b, nch, lc, g, S5_CH).transpose(3, 1, 0, 2, 4).reshape(g, m, kw)
    y_loc, w_end = pl.pallas_call(
        _s5_local_kernel,
        grid=(g,),
        in_specs=[pl.BlockSpec((None, m, kw), lambda i: (i, 0, 0)),
                  pl.BlockSpec((None, kw, kw + 2 * p2), lambda i: (i, 0, 0))],
        out_specs=[pl.BlockSpec((None, m, kw), lambda i: (i, 0, 0)),
                   pl.BlockSpec((None, m, 2 * p2), lambda i: (i, 0, 0))],
        out_shape=[jax.ShapeDtypeStruct((g, m, kw), F32), jax.ShapeDtypeStruct((g, m, 2 * p2), F32)],
        compiler_params=_cparams(1),
        name="s5_local",
    )(u, r_mat)
    gb = 4 if g % 4 == 0 else 1
    x0 = pl.pallas_call(
        functools.partial(_s5_scan_kernel, nb=b, n_chunks=nch, n_ctx_chunks=ncc),
        grid=(g // gb,),
        in_specs=[pl.BlockSpec((gb, m, 2 * p2), lambda i: (i, 0, 0)),
                  pl.BlockSpec((gb, 2, p2), lambda i: (i, 0, 0))],
        out_specs=pl.BlockSpec((gb, m, 2 * p2), lambda i: (i, 0, 0)),
        out_shape=jax.ShapeDtypeStruct((g, m, 2 * p2), BF16),
        compiler_params=_cparams(1),
        name="s5_scan",
    )(w_end, lam)
    y = pl.pallas_call(
        _s5_out_kernel,
        grid=(g,),
        in_specs=[pl.BlockSpec((None, m, kw), lambda i: (i, 0, 0)),
                  pl.BlockSpec((None, m, 2 * p2), lambda i: (i, 0, 0)),
                  pl.BlockSpec((None, 2 * p2, kw), lambda i: (i, 0, 0))],
        out_specs=pl.BlockSpec((None, m, kw), lambda i: (i, 0, 0)),
        out_shape=jax.ShapeDtypeStruct((g, m, kw), BF16),
        compiler_params=_cparams(1),
        name="s5_out",
    )(y_loc, x0, v_mat)
    return y.reshape(g, nch, b, lc, S5_CH).transpose(2, 1, 3, 0, 4).reshape(b, s, d)


def _rope_tables(seq, n_ctx, n_q_heads, n_kv_heads):
    half = ATT_HD // 2
    quarter = half // 2
    inv_freq = ROPE_BASE ** (-np.arange(quarter, dtype=np.float64) / quarter)
    t = np.arange(seq)
    ang_r = (t // GRID_W)[:, None] * inv_freq
    ang_c = (t % GRID_W)[:, None] * inv_freq
    cos_h = np.concatenate([np.cos(ang_r), np.cos(ang_r), np.cos(ang_c), np.cos(ang_c)], axis=1)
    sin_h = np.concatenate([-np.sin(ang_r), np.sin(ang_r), -np.sin(ang_c), np.sin(ang_c)], axis=1)
    cos_h = np.concatenate([np.ones((n_ctx, ATT_HD)), cos_h], 0)
    sin_h = np.concatenate([np.zeros((n_ctx, ATT_HD)), sin_h], 0)
    scale = ATT_HD ** -0.5
    s = seq + n_ctx
    cos = np.concatenate([np.tile(cos_h, (1, n_q_heads)) * scale, np.tile(cos_h, (1, n_kv_heads)),
                          np.ones((s, n_kv_heads * ATT_HD))], axis=1)
    sin = np.concatenate([np.tile(sin_h, (1, n_q_heads)) * scale, np.tile(sin_h, (1, n_kv_heads)),
                          np.zeros((s, n_kv_heads * ATT_HD))], axis=1)
    return jnp.asarray(cos, F32), jnp.asarray(sin, F32)


def _rope_kernel(x_ref, cos_ref, sin_ref, o_ref):
    x = x_ref[...]
    n = x.shape[-1]
    quarter = ATT_HD // 4
    lane = lax.broadcasted_iota(jnp.int32, x.shape, 1)
    first = (lane & (2 * quarter - 1)) < quarter
    partner = jnp.where(first, pltpu.roll(x, n - quarter, 1), pltpu.roll(x, quarter, 1))
    o_ref[...] = (x * cos_ref[...] + partner * sin_ref[...]).astype(o_ref.dtype)


def _rope(qkv, cos, sin, tm=256):
    b, s, w = qkv.shape
    return pl.pallas_call(
        _rope_kernel,
        grid=(b, s // tm),
        in_specs=[pl.BlockSpec((None, tm, w), lambda i, j: (i, j, 0)),
                  pl.BlockSpec((tm, w), lambda i, j: (j, 0)),
                  pl.BlockSpec((tm, w), lambda i, j: (j, 0))],
        out_specs=pl.BlockSpec((None, tm, w), lambda i, j: (i, j, 0)),
        out_shape=jax.ShapeDtypeStruct((b, s, w), BF16),
        compiler_params=_cparams(2),
        name="rope",
    )(qkv, cos, sin)


def _attn_kernel(sink_ref, q_ref, kc_ref, kp_ref, kk_ref, kn_ref, vc_ref, vp_ref, vk_ref, vn_ref, o_ref,
                 *, n_ctx_blk, seq):
    blk = ATT_BLK
    gp = pl.program_id(1)
    jb = pl.program_id(2)
    n_ctx = kc_ref.shape[0]
    k_all = jnp.concatenate([kc_ref[...], kp_ref[...], kk_ref[...], kn_ref[...]], axis=0)
    v_all = jnp.concatenate([vc_ref[...], vp_ref[...], vk_ref[...], vn_ref[...]], axis=0)
    nk = n_ctx + 3 * blk
    col = lax.broadcasted_iota(jnp.int32, (blk, nk), 1)
    q_pos = (jb - n_ctx_blk) * blk + lax.broadcasted_iota(jnp.int32, (blk, nk), 0)
    k_pos = (jb - n_ctx_blk - 1) * blk + (col - n_ctx)
    local_ok = (jb >= n_ctx_blk) & (k_pos >= 0) & (k_pos < seq) & (jnp.abs(q_pos - k_pos) <= blk)
    valid = (col < n_ctx) | local_ok
    q = q_ref[...]
    outs = []
    for kv in range(2):
        k_h = k_all[:, kv * ATT_HD:(kv + 1) * ATT_HD]
        v_h = v_all[:, kv * ATT_HD:(kv + 1) * ATT_HD]
        for r in range(ATT_REP):
            hq = kv * ATT_REP + r
            s = _dot_nt(q[:, hq * ATT_HD:(hq + 1) * ATT_HD], k_h)
            s = jnp.where(valid, s, -jnp.inf)
            sink = sink_ref[gp * 2 * ATT_REP + hq]
            m = jnp.maximum(jnp.max(s, axis=-1, keepdims=True), sink)
            e = jnp.exp(s - m)
            p = e / (jnp.sum(e, axis=-1, keepdims=True) + jnp.exp(sink - m))
            outs.append(_dot(p.astype(BF16), v_h))
    o_ref[...] = jnp.concatenate(outs, axis=-1).astype(o_ref.dtype)


def _attention(qkv, sink, n_ctx, seq):
    b, s, w = qkv.shape
    n_kv = w // (ATT_HD * (ATT_REP + 2))
    n_q = n_kv * ATT_REP
    blk = ATT_BLK
    nblk, ncb = s // blk, n_ctx // blk
    qw = 2 * ATT_REP * ATT_HD
    k0 = n_q * ATT_HD // blk
    v0 = k0 + n_kv * ATT_HD // blk

    def ctx_spec(c0):
        return pl.BlockSpec((None, n_ctx, blk), lambda i, g, j: (i, 0, c0 + g))

    def loc_spec(c0, off):
        return pl.BlockSpec((None, blk, blk), lambda i, g, j: (i, jnp.clip(j + off, ncb, nblk - 1), c0 + g))

    return pl.pallas_call(
        functools.partial(_attn_kernel, n_ctx_blk=ncb, seq=seq),
        grid=(b, n_kv // 2, nblk),
        in_specs=[pl.BlockSpec(memory_space=pltpu.SMEM),
                  pl.BlockSpec((None, blk, qw), lambda i, g, j: (i, j, g)),
                  ctx_spec(k0), loc_spec(k0, -1), loc_spec(k0, 0), loc_spec(k0, 1),
                  ctx_spec(v0), loc_spec(v0, -1), loc_spec(v0, 0), loc_spec(v0, 1)],
        out_specs=pl.BlockSpec((None, blk, qw), lambda i, g, j: (i, j, g)),
        out_shape=jax.ShapeDtypeStruct((b, s, n_q * ATT_HD), BF16),
        compiler_params=_cparams(3),
        name="attention",
    )(sink.astype(F32), qkv, qkv, qkv, qkv, qkv, qkv, qkv, qkv, qkv)


def _router_kernel(x_ref, g_ref, shl_ref, shc_ref, scl_ref, scc_ref, wr_ref, br_ref, h_ref, slab_ref,
                   *, tm, n_ctx, n_experts):
    is_ctx = _row_is_ctx(pl.program_id(1), tm, n_ctx)
    h = _norm_mod(x_ref[...], g_ref[...], shl_ref[...], scl_ref[...], shc_ref[...], scc_ref[...], is_ctx)
    h_ref[...] = h.astype(h_ref.dtype)
    logits = jnp.dot(h, wr_ref[...], precision=HIGHEST, preferred_element_type=F32) + br_ref[...]
    lanes = logits.shape[-1]
    lane = lax.broadcasted_iota(jnp.int32, (tm, lanes), 1)
    lane_f = lane.astype(F32)
    per = n_experts // MOE_GROUPS
    big = float(lanes)
    neg = -jnp.inf
    gl = jnp.where(lane < MOE_GROUPS, logits, neg)
    gmax = jnp.max(gl, axis=-1, keepdims=True)
    p_sel = 1.0 / jnp.sum(jnp.exp(gl - gmax), axis=-1, keepdims=True)
    g_sel = jnp.min(jnp.where(gl == gmax, lane_f, big), axis=-1, keepdims=True)
    lo = MOE_GROUPS + g_sel * per
    emask = (lane_f >= lo) & (lane_f < lo + per)
    el = jnp.where(emask, logits, neg)
    v1 = jnp.max(el, axis=-1, keepdims=True)
    i1 = jnp.min(jnp.where(el == v1, lane_f, big), axis=-1, keepdims=True)
    el2 = jnp.where(lane_f == i1, neg, el)
    v2 = jnp.max(el2, axis=-1, keepdims=True)
    i2 = jnp.min(jnp.where(el2 == v2, lane_f, big), axis=-1, keepdims=True)
    t = jnp.exp(v2 - v1)
    w1 = p_sel / (1.0 + t)
    w2 = p_sel * t / (1.0 + t)
    slab = jnp.where(lane == 0, i1 - MOE_GROUPS,
                     jnp.where(lane == 1, i2 - MOE_GROUPS,
                               jnp.where(lane == 2, w1, jnp.where(lane == 3, w2, 0.0))))
    slab_ref[...] = slab


def _router(xs, g, mod, n_ctx, w_group, b_group, w_expert, b_expert, tm=256):
    b, s, d = xs.shape
    n_experts = w_expert.shape[1]
    lanes = 128
    wr = jnp.zeros((d, lanes), F32).at[:, :MOE_GROUPS].set(w_group).at[:, MOE_GROUPS:MOE_GROUPS + n_experts].set(w_expert)
    br = jnp.zeros((1, lanes), F32).at[0, :MOE_GROUPS].set(b_group).at[0, MOE_GROUPS:MOE_GROUPS + n_experts].set(b_expert)
    return pl.pallas_call(
        functools.partial(_router_kernel, tm=tm, n_ctx=n_ctx, n_experts=n_experts),
        grid=(b, s // tm),
        in_specs=[pl.BlockSpec((None, tm, d), lambda i, j: (i, j, 0)),
                  pl.BlockSpec((1, d), lambda i, j: (0, 0))]
                 + _mod_specs(b, d, 3, 0) + _mod_specs(b, d, 4, 0)
                 + [pl.BlockSpec((d, lanes), lambda i, j: (0, 0)),
                    pl.BlockSpec((1, lanes), lambda i, j: (0, 0))],
        out_specs=[pl.BlockSpec((None, tm, d), lambda i, j: (i, j, 0)),
                   pl.BlockSpec((None, tm, lanes), lambda i, j: (i, j, 0))],
        out_shape=[jax.ShapeDtypeStruct((b, s, d), BF16), jax.ShapeDtypeStruct((b, s, lanes), F32)],
        compiler_params=_cparams(2),
        name="router",
    )(xs, g.reshape(1, d), mod, mod, mod, mod, wr, br)


def _expert_kernel(te_ref, nu_ref, x_ref, wgu_ref, wd_ref, y_ref, wgub_ref, wdb_ref):
    i = pl.program_id(0)
    used = i < nu_ref[0]

    @pl.when(used)
    def _():
        new_expert = (i == 0) | (te_ref[i] != te_ref[jnp.maximum(i - 1, 0)])

        @pl.when(new_expert)
        def _():
            wgub_ref[...] = wgu_ref[...].astype(BF16)
            wdb_ref[...] = wd_ref[...].astype(BF16)

        ab = _dot(x_ref[...], wgub_ref[...])
        f = ab.shape[-1] // 2
        mid = (_silu(ab[:, :f]) * ab[:, f:]).astype(BF16)
        y_ref[...] = _dot(mid, wdb_ref[...]).astype(y_ref.dtype)

    @pl.when(jnp.logical_not(used))
    def _():
        y_ref[...] = jnp.zeros_like(y_ref)


def _experts(xg, tile_expert, n_used, w_gate_up, w_down, tm):
    p, d = xg.shape
    n_exp, _, f2 = w_gate_up.shape
    f = f2 // 2
    return pl.pallas_call(
        _expert_kernel,
        grid_spec=pltpu.PrefetchScalarGridSpec(
            num_scalar_prefetch=2,
            grid=(p // tm,),
            in_specs=[pl.BlockSpec((tm, d), lambda i, te, nu: (i, 0)),
                      pl.BlockSpec((None, d, f2), lambda i, te, nu: (te[i], 0, 0)),
                      pl.BlockSpec((None, f, d), lambda i, te, nu: (te[i], 0, 0))],
            out_specs=pl.BlockSpec((tm, d), lambda i, te, nu: (i, 0)),
            scratch_shapes=[pltpu.VMEM((d, f2), BF16), pltpu.VMEM((f, d), BF16)]),
        out_shape=jax.ShapeDtypeStruct((p, d), BF16),
        compiler_params=_cparams(1),
        name="experts",
    )(tile_expert, n_used, xg, w_gate_up, w_down)


def _combine_kernel(xs_ref, slab_ref, y1_ref, y2_ref, gl_ref, gc_ref, o_ref, *, tm, n_ctx):
    gate = jnp.where(_row_is_ctx(pl.program_id(1), tm, n_ctx), gc_ref[...], gl_ref[...])
    slab = slab_ref[...]
    f = slab[:, 2:3] * y1_ref[...].astype(F32) + slab[:, 3:4] * y2_ref[...].astype(F32)
    o_ref[...] = xs_ref[...] + gate * f


def _combine(xs, slab, y1, y2, mod, n_ctx, tm=256):
    b, s, d = xs.shape
    row = lambda w: pl.BlockSpec((None, tm, w), lambda i, j: (i, j, 0))
    return pl.pallas_call(
        functools.partial(_combine_kernel, tm=tm, n_ctx=n_ctx),
        grid=(b, s // tm),
        in_specs=[row(d), row(slab.shape[-1]), row(d), row(d)] + _mod_specs(b, d, 5, 0),
        out_specs=row(d),
        out_shape=jax.ShapeDtypeStruct((b, s, d), F32),
        compiler_params=_cparams(2),
        name="moe_combine",
    )(xs, slab, y1, y2, mod, mod)


def _moe(xs, g, mod, n_ctx, w_group, b_group, w_expert, b_expert, w_gate_up, w_down):
    b, s, d = xs.shape
    n_exp = w_expert.shape[1]
    tm = MOE_TM
    h, slab = _router(xs, g, mod, n_ctx, w_group, b_group, w_expert, b_expert)
    t = b * s
    n_pair = t * MOE_TOP_K
    flat_e = slab[:, :, :MOE_TOP_K].astype(jnp.int32).reshape(n_pair)
    order = jnp.argsort(flat_e, stable=True).astype(jnp.int32)
    sorted_e = flat_e[order]
    counts = jnp.zeros((n_exp,), jnp.int32).at[flat_e].add(1)
    tiles = (counts + tm - 1) // tm
    tile_end = jnp.cumsum(tiles)
    pad_start = (tile_end - tiles) * tm
    grp_start = jnp.cumsum(counts) - counts
    dest = pad_start[sorted_e] + jnp.arange(n_pair, dtype=jnp.int32) - grp_start[sorted_e]
    n_tiles = n_pair // tm + n_exp
    src_tok = jnp.zeros((n_tiles * tm,), jnp.int32).at[dest].set(order // MOE_TOP_K)
    pos = jnp.zeros((n_pair,), jnp.int32).at[order].set(dest).reshape(t, MOE_TOP_K)
    tile_expert = jnp.minimum(jnp.searchsorted(tile_end, jnp.arange(n_tiles, dtype=jnp.int32), side="right"),
                              n_exp - 1).astype(jnp.int32)
    n_used = tile_end[-1:].astype(jnp.int32)
    xg = jnp.take(h.reshape(t, d), src_tok, axis=0)
    y = _experts(xg, tile_expert, n_used, w_gate_up, w_down, tm)
    y1 = jnp.take(y, pos[:, 0], axis=0).reshape(b, s, d)
    y2 = jnp.take(y, pos[:, 1], axis=0).reshape(b, s, d)
    return _combine(xs, slab, y1, y2, mod, n_ctx)


def _final_norm_kernel(x_ref, g_ref, o_ref):
    x = x_ref[...]
    o_ref[...] = x * lax.rsqrt(jnp.mean(x * x, axis=-1, keepdims=True) + RMS_EPS) * g_ref[...]


def _final_norm(xs, g, n_ctx, tm=256):
    b, s, d = xs.shape
    off = n_ctx // tm
    return pl.pallas_call(
        _final_norm_kernel,
        grid=(b, (s - n_ctx) // tm),
        in_specs=[pl.BlockSpec((None, tm, d), lambda i, j: (i, j + off, 0)),
                  pl.BlockSpec((1, d), lambda i, j: (0, 0))],
        out_specs=pl.BlockSpec((None, tm, d), lambda i, j: (i, j, 0)),
        out_shape=jax.ShapeDtypeStruct((b, s - n_ctx, d), F32),
        compiler_params=_cparams(2),
        name="final_norm",
    )(xs, g.reshape(1, d))


def kernel(x, c, ctx, c_ctx, ada_w, ada_b, norm_mix_g, norm_ffn_g, final_norm_g, hgrn_w_in, hgrn_w_out, hgrn_gnorm_g, hgrn_lb_logits, s5_a_re, s5_a_im, s5_log_dt, s5_b_re, s5_b_im, s5_c_re, s5_c_im, s5_d, s5_w_glu, attn_w_qkv, attn_w_o, attn_sink, moe_w_group, moe_b_group, moe_w_expert, moe_b_expert, moe_w_gate_up, moe_w_down):
    bsz, seq, d = x.shape
    n_ctx = ctx.shape[1]
    depth = ada_w.shape[0]
    n_q_heads = d // ATT_HD
    n_kv_heads = n_q_heads // ATT_REP

    lb_cum = jnp.cumsum(jax.nn.softmax(hgrn_lb_logits.astype(F32), axis=0), axis=0)
    lower_bounds = lb_cum - lb_cum[:1]

    r = -(-(bsz + 1) // 8) * 8
    c_all = jnp.zeros((r, d), F32).at[:bsz].set(c).at[bsz].set(c_ctx)
    mod_all = _adaln(c_all, ada_w, ada_b).reshape(depth, r, 1, 6 * d)

    xs = jnp.concatenate([ctx, x], axis=1)
    for i in range(depth):
        mod = mod_all[i]
        h = _normmod(xs, norm_mix_g[i], mod, 0, n_ctx)
        kind, j = i % 3, i // 3
        if kind == 0:
            proj = _matmul(h, hgrn_w_in[j])
            o = _hgrn_scan(proj, lower_bounds[i], hgrn_gnorm_g[j], n_ctx)
            xs = _matmul(o, hgrn_w_out[j], mode="res", xs=xs, mod=mod, sec_gate=2, n_ctx=n_ctx)
        elif kind == 1:
            mats = _s5_matrices(s5_a_re[j], s5_a_im[j], s5_log_dt[j], s5_b_re[j], s5_b_im[j],
                                s5_c_re[j], s5_c_im[j], s5_d[j])
            y = _s5_mix(h, *mats, n_ctx)
            xs = _matmul(y, s5_w_glu[j], mode="glu", xs=xs, mod=mod, sec_gate=2, n_ctx=n_ctx)
        else:
            qkv = _matmul(h, attn_w_qkv[j])
            cos, sin = _rope_tables(seq, n_ctx, n_q_heads, n_kv_heads)
            o = _attention(_rope(qkv, cos, sin), attn_sink[j], n_ctx, seq)
            xs = _matmul(o, attn_w_o[j], mode="res", xs=xs, mod=mod, sec_gate=2, n_ctx=n_ctx)
        xs = _moe(xs, norm_ffn_g[i], mod, n_ctx, moe_w_group[i], moe_b_group[i], moe_w_expert[i],
                  moe_b_expert[i], moe_w_gate_up[i], moe_w_down[i])
    return _final_norm(xs, final_norm_g, n_ctx)
```
